```python
import jax
import jax.numpy as jnp
from jax import lax
import numpy as np

D_MODEL = 2048
BATCH = 1
SEQ = 8192
DEPTH = 1

CHUNK = 64
PLE_DIM = 256
D_FF = 5632
GDN_HEADS = 8
GDN_DK = 128
GDN_DV = 128
GDN_QKV = 2 * GDN_HEADS * GDN_DK + GDN_HEADS * GDN_DV
GDN_W = GDN_HEADS * GDN_DV
CONV_K = 4
ATT_HEADS = 8
ATT_DH = 128
ATT_W = ATT_HEADS * ATT_DH
LEFT_CHUNKS = 8
BAND = (LEFT_CHUNKS + 1) * CHUNK
MAX_REL = 128
N_REL = (CHUNK - 1) + MAX_REL + 1
EPS = 1e-6
NEG_INF = -1e30
IN_SPLITS = (GDN_QKV, GDN_W, GDN_HEADS, GDN_HEADS, 3 * ATT_W, D_MODEL, D_MODEL)
IN_COLS = sum(IN_SPLITS)

kernel_name = "hybrid_gdn_bandattn_macaron_block"


def rmsnorm(x, w, eps=EPS):
    xf = x.astype(jnp.float32)
    y = xf * lax.rsqrt(jnp.mean(xf * xf, axis=-1, keepdims=True) + eps)
    return (y * w.astype(jnp.float32)).astype(x.dtype)


def l2norm(x, eps=EPS):
    xf = x.astype(jnp.float32)
    return xf * lax.rsqrt(jnp.sum(xf * xf, axis=-1, keepdims=True) + eps)


def swiglu_ffn(h, w_gu, w_down):
    g, u = jnp.split(h @ w_gu, 2, axis=-1)
    return (jax.nn.silu(g) * u) @ w_down


def causal_short_conv(x, w):
    ksz = w.shape[0]
    seqlen = x.shape[1]
    xp = jnp.pad(x, ((0, 0), (ksz - 1, 0), (0, 0)))
    y = xp[:, 0:seqlen] * w[0]
    for j in range(1, ksz):
        y = y + xp[:, j:j + seqlen] * w[j]
    return y


def to_chunks(t):
    b, s, h, d = t.shape
    return t.reshape(b, s // CHUNK, CHUNK, h, d).transpose(0, 3, 1, 2, 4)


def to_chunks_h(t):
    b, s, h = t.shape
    return t.reshape(b, s // CHUNK, CHUNK, h).transpose(0, 3, 1, 2)


def gated_delta_rule_chunked(q, k, v, g, beta):
    c = q.shape[-2]
    dv = v.shape[-1]
    gc = jnp.cumsum(g, axis=-1)
    idx = jnp.arange(c)
    incl = idx[:, None] >= idx[None, :]
    strict = idx[:, None] > idx[None, :]
    diff = gc[..., :, None] - gc[..., None, :]
    decay = jnp.where(incl, jnp.exp(jnp.where(incl, diff, 0.0)), 0.0)
    kb = k * beta[..., None]
    lmat = jnp.where(strict, jnp.einsum('bhnid,bhnjd->bhnij', kb, k) * decay, 0.0)
    rhs = jnp.concatenate([v * beta[..., None], kb * jnp.exp(gc)[..., None]], axis=-1)
    sol = lax.linalg.triangular_solve(lmat + jnp.eye(c, dtype=lmat.dtype), rhs,
                                      left_side=True, lower=True, unit_diagonal=True)
    u, w = sol[..., :dv], sol[..., dv:]
    aqk = jnp.einsum('bhnid,bhnjd->bhnij', q, k) * decay
    q_dec = q * jnp.exp(gc)[..., None]
    k_tail = k * jnp.exp(gc[..., -1:] - gc)[..., None]
    tail = jnp.exp(gc[..., -1])

    def step(state, xs):
        u_c, w_c, aqk_c, qd_c, kt_c, tl_c = xs
        v_new = u_c - jnp.einsum('bhcd,bhde->bhce', w_c, state)
        o_c = (jnp.einsum('bhcd,bhde->bhce', qd_c, state)
               + jnp.einsum('bhcj,bhje->bhce', aqk_c, v_new))
        state = state * tl_c[..., None, None] + jnp.einsum('bhcd,bhce->bhde', kt_c, v_new)
        return state, o_c

    xs = tuple(jnp.moveaxis(t, 2, 0) for t in (u, w, aqk, q_dec, k_tail, tail))
    s0 = jnp.zeros(q.shape[:2] + (q.shape[-1], dv), jnp.float32)
    _, o = lax.scan(step, s0, xs)
    return jnp.moveaxis(o, 0, 2)


def gated_deltanet_branch(qkv, z, a_raw, b_raw, conv_w, a_log, dt_bias, norm_w):
    bsz, seqlen, _ = qkv.shape
    qkv = jax.nn.silu(causal_short_conv(qkv, conv_w))
    q, k, v = jnp.split(qkv, [GDN_HEADS * GDN_DK, 2 * GDN_HEADS * GDN_DK], axis=-1)
    q = l2norm(q.reshape(bsz, seqlen, GDN_HEADS, GDN_DK)) * (GDN_DK ** -0.5)
    k = l2norm(k.reshape(bsz, seqlen, GDN_HEADS, GDN_DK))
    v = v.reshape(bsz, seqlen, GDN_HEADS, GDN_DV).astype(jnp.float32)
    g = -jnp.exp(a_log.astype(jnp.float32)) * jax.nn.softplus(
        a_raw.astype(jnp.float32) + dt_bias.astype(jnp.float32))
    beta = jax.nn.sigmoid(b_raw.astype(jnp.float32))
    o = gated_delta_rule_chunked(to_chunks(q), to_chunks(k), to_chunks(v),
                                 to_chunks_h(g), to_chunks_h(beta))
    o = o.transpose(0, 2, 3, 1, 4).reshape(bsz, seqlen, GDN_HEADS, GDN_DV)
    zf = z.reshape(bsz, seqlen, GDN_HEADS, GDN_DV).astype(jnp.float32)
    o = rmsnorm(o, norm_w) * jax.nn.silu(zf)
    return o.reshape(bsz, seqlen, GDN_W).astype(z.dtype)


def gather_band(t):
    nc = t.shape[1]
    tp = jnp.pad(t, ((0, 0), (LEFT_CHUNKS, 0), (0, 0), (0, 0), (0, 0)))
    return jnp.concatenate([tp[:, j:j + nc] for j in range(LEFT_CHUNKS + 1)], axis=2)


def band_attention_branch(qkv, q_norm_w, k_norm_w, rel_bias):
    bsz, seqlen, _ = qkv.shape
    nc = seqlen // CHUNK
    q, k, v = jnp.split(qkv, 3, axis=-1)
    shp = (bsz, nc, CHUNK, ATT_HEADS, ATT_DH)
    q = rmsnorm(q.reshape(shp), q_norm_w)
    k_band = gather_band(rmsnorm(k.reshape(shp), k_norm_w))
    v_band = gather_band(v.reshape(shp))
    qpos = np.arange(CHUNK)[:, None]
    kpos = np.arange(BAND)[None, :] - LEFT_CHUNKS * CHUNK
    rel_idx = np.clip(qpos - kpos, -(CHUNK - 1), MAX_REL) + (CHUNK - 1)
    bias = rel_bias[:, rel_idx].astype(jnp.float32)
    valid = jnp.asarray((np.arange(nc)[:, None] - LEFT_CHUNKS
                         + np.arange(BAND)[None, :] // CHUNK) >= 0)
    s = jnp.einsum('bnqhd,bnkhd->bhnqk', q, k_band).astype(jnp.float32) * (ATT_DH ** -0.5)
    s = jnp.where(valid[:, None, :], s + bias[:, None], NEG_INF)
    pr = jax.nn.softmax(s, axis=-1).astype(v_band.dtype)
    o = jnp.einsum('bhnqk,bnkhd->bnqhd', pr, v_band)
    return o.reshape(bsz, seqlen, ATT_W)


def setup_inputs(seed: int = 0) -> dict:
    key = jax.random.key(seed)
    ks = jax.random.split(key, 24)
    f32 = jnp.float32

    def dense(k, shape):
        return jax.random.normal(k, shape, f32) * (shape[-2] ** -0.5)

    def gain(k, n):
        return 1.0 + 0.05 * jax.random.normal(k, (DEPTH, n), f32)

    dt = jnp.exp(jax.random.uniform(ks[9], (DEPTH, GDN_HEADS), f32,
                                    float(np.log(1e-3)), float(np.log(1e-1))))
    dt_bias = dt + jnp.log(-jnp.expm1(-dt))
    return {
        'x': jax.random.normal(ks[0], (BATCH, SEQ, D_MODEL), f32),
        'p': jax.random.normal(ks[1], (DEPTH, BATCH, SEQ, PLE_DIM), f32),
        'ffn1_norm': gain(ks[2], D_MODEL),
        'ffn1_w_gu': dense(ks[3], (DEPTH, D_MODEL, 2 * D_FF)),
        'ffn1_w_down': dense(ks[4], (DEPTH, D_FF, D_MODEL)),
        'mix_norm': gain(ks[5], D_MODEL),
        'w_in': dense(ks[6], (DEPTH, D_MODEL, IN_COLS)),
        'conv_w': 0.5 * jax.random.normal(ks[7], (DEPTH, CONV_K, GDN_QKV), f32),
        'a_log': jnp.log(jax.random.uniform(ks[8], (DEPTH, GDN_HEADS), f32, 1.0, 16.0)),
        'dt_bias': dt_bias,
        'gdn_norm': gain(ks[10], GDN_DV),
        'q_norm': gain(ks[11], ATT_DH),
        'k_norm': gain(ks[12], ATT_DH),
        'rel_bias': 0.5 * jax.random.normal(ks[13], (DEPTH, ATT_HEADS, N_REL), f32),
        'w_branch_a': dense(ks[14], (DEPTH, GDN_W, D_MODEL)),
        'w_branch_b': dense(ks[15], (DEPTH, ATT_W, D_MODEL)),
        'w_out': dense(ks[16], (DEPTH, D_MODEL, D_MODEL)),
        'ffn2_norm': gain(ks[17], D_MODEL),
        'ffn2_w_gu': dense(ks[18], (DEPTH, D_MODEL, 2 * D_FF)),
        'ffn2_w_down': dense(ks[19], (DEPTH, D_FF, D_MODEL)),
        'ple_norm': gain(ks[20], D_MODEL),
        'ple_gate': dense(ks[21], (DEPTH, D_MODEL, D_MODEL)),
        'ple_proj': dense(ks[22], (DEPTH, PLE_DIM, D_MODEL)),
    }


def reference(x, p, ffn1_norm, ffn1_w_gu, ffn1_w_down, mix_norm, w_in, conv_w, a_log,
              dt_bias, gdn_norm, q_norm, k_norm, rel_bias, w_branch_a, w_branch_b, w_out,
              ffn2_norm, ffn2_w_gu, ffn2_w_down, ple_norm, ple_gate, ple_proj):
    split_at = [int(s) for s in np.cumsum(IN_SPLITS)[:-1]]
    for i in range(DEPTH):
        x = x + 0.5 * swiglu_ffn(rmsnorm(x, ffn1_norm[i]), ffn1_w_gu[i], ffn1_w_down[i])
        h = rmsnorm(x, mix_norm[i])
        qkv_a, z_a, a_raw, b_raw, qkv_b, gate_a, gate_b = jnp.split(h @ w_in[i], split_at, axis=-1)
        ya = gated_deltanet_branch(qkv_a, z_a, a_raw, b_raw, conv_w[i], a_log[i],
                                   dt_bias[i], gdn_norm[i])
        yb = band_attention_branch(qkv_b, q_norm[i], k_norm[i], rel_bias[i])
        mixed = (jax.nn.sigmoid(gate_a) * (ya @ w_branch_a[i])
                 + jax.nn.sigmoid(gate_b) * (yb @ w_branch_b[i]))
        x = x + mixed @ w_out[i]
        x = x + 0.5 * swiglu_ffn(rmsnorm(x, ffn2_norm[i]), ffn2_w_gu[i], ffn2_w_down[i])
        x = x + jax.nn.sigmoid(rmsnorm(x, ple_norm[i]) @ ple_gate[i]) * (p[i] @ ple_proj[i])
    return x
```

```python
import functools

import jax
import jax.numpy as jnp
import numpy as np
from jax import lax
from jax.experimental import pallas as pl
from jax.experimental.pallas import tpu as pltpu

F32 = jnp.float32
BF16 = jnp.bfloat16

CHUNK = 64
CHUNK_SHIFT = 6
GDN_HEADS = 8
GDN_D = 128
GDN_W = GDN_HEADS * GDN_D
CONV_K = 4
ATT_HEADS = 8
ATT_DH = 128
ATT_W = ATT_HEADS * ATT_DH
LEFT_CHUNKS = 8
MAX_REL = 128
EPS = 1e-6
NEG_INF = -1e30
LANES = 128
HIST = 8

VMEM_LIMIT = 56 * 1024 * 1024

D_MODEL = 2048
COL_QKV_A = 0
COL_Z = 3 * GDN_W
COL_GATES = COL_Z + GDN_W
COL_QKV_B = COL_GATES + 2 * D_MODEL
W_IN_AB = COL_Z + GDN_W


def _rmsnorm(x, w):
    ms = jnp.mean(x * x, axis=-1, keepdims=True)
    return x * lax.rsqrt(ms + EPS) * w


def _sigmoid(x):
    return 1.0 / (1.0 + jnp.exp(-x))


def _dot(a, b):
    return jnp.dot(a.astype(BF16), b.astype(BF16), preferred_element_type=F32)


def _dot_nt(a, b):
    return lax.dot_general(a.astype(BF16), b.astype(BF16), (((1,), (1,)), ((), ())),
                           preferred_element_type=F32)


def _dot_tn(a, b):
    return lax.dot_general(a.astype(BF16), b.astype(BF16), (((0,), (0,)), ((), ())),
                           preferred_element_type=F32)


def _dot_hi(a, b):
    return jnp.dot(a, b, preferred_element_type=F32, precision=lax.Precision.HIGHEST)


def _resident(shape):
    return pl.BlockSpec(shape, lambda *_: (0,) * len(shape), pipeline_mode=pl.Buffered(1))


def _params(sem):
    return pltpu.CompilerParams(dimension_semantics=sem, vmem_limit_bytes=VMEM_LIMIT)


def _ffn_kernel(x_ref, nw_ref, wg_ref, wu_ref, wd_ref, o_ref, h_ref):
    j = pl.program_id(1)

    @pl.when(j == 0)
    def _():
        x = x_ref[...]
        h_ref[...] = _rmsnorm(x, nw_ref[...]).astype(BF16)
        o_ref[...] = x

    h = h_ref[...]
    g = jnp.dot(h, wg_ref[...], preferred_element_type=F32)
    u = jnp.dot(h, wu_ref[...], preferred_element_type=F32)
    a = (0.5 * g * _sigmoid(g)) * u
    o_ref[...] += jnp.dot(a.astype(BF16), wd_ref[...], preferred_element_type=F32)


def _ffn(x, norm_w, w_gu, w_down, tm, tf):
    m, d = x.shape
    dff = w_down.shape[0]
    nj = dff // tf
    return pl.pallas_call(
        _ffn_kernel,
        grid=(m // tm, nj),
        in_specs=[
            pl.BlockSpec((tm, d), lambda i, j: (i, 0)),
            pl.BlockSpec((1, d), lambda i, j: (0, 0)),
            pl.BlockSpec((d, tf), lambda i, j: (0, j)),
            pl.BlockSpec((d, tf), lambda i, j: (0, j + nj)),
            pl.BlockSpec((tf, d), lambda i, j: (j, 0)),
        ],
        out_specs=pl.BlockSpec((tm, d), lambda i, j: (i, 0)),
        out_shape=jax.ShapeDtypeStruct((m, d), F32),
        scratch_shapes=[pltpu.VMEM((tm, d), BF16)],
        compiler_params=_params(("parallel", "arbitrary")),
        name="ffn",
    )(x, norm_w, w_gu, w_gu, w_down)


def _proj_kernel(x_ref, nw_ref, w_ref, wab_ref, o_ref, ab_ref, h_ref):
    j = pl.program_id(1)

    @pl.when(j == 0)
    def _():
        h = _rmsnorm(x_ref[...], nw_ref[...]).astype(BF16)
        h_ref[...] = h
        ab_ref[...] = jnp.dot(h, wab_ref[...], preferred_element_type=F32)

    o_ref[...] = jnp.dot(h_ref[...], w_ref[...], preferred_element_type=F32)


def _proj(x, norm_w, w_main, w_ab, tm, tn):
    m, d = x.shape
    n = w_main.shape[1]
    return pl.pallas_call(
        _proj_kernel,
        grid=(m // tm, n // tn),
        in_specs=[
            pl.BlockSpec((tm, d), lambda i, j: (i, 0)),
            pl.BlockSpec((1, d), lambda i, j: (0, 0)),
            pl.BlockSpec((d, tn), lambda i, j: (0, j)),
            pl.BlockSpec((d, LANES), lambda i, j: (0, 0)),
        ],
        out_specs=[
            pl.BlockSpec((tm, tn), lambda i, j: (i, j)),
            pl.BlockSpec((tm, LANES), lambda i, j: (i, 0)),
        ],
        out_shape=[jax.ShapeDtypeStruct((m, n), F32),
                   jax.ShapeDtypeStruct((m, LANES), F32)],
        scratch_shapes=[pltpu.VMEM((tm, d), BF16)],
        compiler_params=_params(("parallel", "arbitrary")),
        name="proj",
    )(x, norm_w, w_main, w_ab)


def _unit_lower_inverse(lm, ii, jj):
    eye = (ii == jj).astype(F32)
    blk16 = (ii >> 4) == (jj >> 4)
    blk32 = (ii >> 5) == (jj >> 5)
    ld = jnp.where(blk16, lm, 0.0)
    l2 = _dot_hi(ld, ld)
    l4 = _dot_hi(l2, l2)
    l8 = _dot_hi(l4, l4)
    dinv = eye - ld
    dinv = dinv + _dot_hi(dinv, l2)
    dinv = dinv + _dot_hi(dinv, l4)
    dinv = dinv + _dot_hi(dinv, l8)
    c32 = jnp.where(blk32 & jnp.logical_not(blk16), lm, 0.0)
    x32 = dinv - _dot_hi(dinv, _dot_hi(c32, dinv))
    c64 = jnp.where(blk32, 0.0, lm)
    return x32 - _dot_hi(x32, _dot_hi(c64, x32))


def _gdn_kernel(qkv_ref, z_ref, ab_ref, cw_ref, alog_ref, dtb_ref, gnw_ref, o_ref, xbuf, s_ref):
    t = pl.program_id(0)
    c = CHUNK

    @pl.when(t == 0)
    def _():
        xbuf[0:HIST, :] = jnp.zeros((HIST, 3 * GDN_W), F32)
        s_ref[...] = jnp.zeros(s_ref.shape, F32)

    xbuf[HIST:HIST + c, :] = qkv_ref[...]
    cw = cw_ref[...]
    base = HIST - (CONV_K - 1)
    y = xbuf[base:base + c, :] * cw[0:1, :]
    for j in range(1, CONV_K):
        y = y + xbuf[base + j:base + j + c, :] * cw[j:j + 1, :]
    xbuf[0:HIST, :] = xbuf[c:c + HIST, :]
    y = y * _sigmoid(y)

    ab = ab_ref[...]
    sp_in = ab + dtb_ref[...]
    softplus = jnp.maximum(sp_in, 0.0) + jnp.log1p(jnp.exp(-jnp.abs(sp_in)))
    g = -jnp.exp(alog_ref[...]) * softplus
    beta = _sigmoid(ab)

    ii = lax.broadcasted_iota(jnp.int32, (c, c), 0)
    jj = lax.broadcasted_iota(jnp.int32, (c, c), 1)
    incl = ii >= jj
    strict = ii > jj
    gc = _dot_hi(incl.astype(F32), g)
    gct = lax.dot_general(g, (ii <= jj).astype(F32), (((0,), (0,)), ((), ())),
                          preferred_element_type=F32,
                          precision=lax.Precision.HIGHEST)
    eg = jnp.exp(gc)
    gc_last = gc[c - 1:c, :]
    egl = jnp.exp(gc_last - gc)
    tail = jnp.exp(gc_last)
    gnw = gnw_ref[...]

    for h in range(GDN_HEADS):
        sl = slice(h * GDN_D, (h + 1) * GDN_D)
        qh = y[:, sl]
        kh = y[:, GDN_W + h * GDN_D:GDN_W + (h + 1) * GDN_D]
        vh = y[:, 2 * GDN_W + h * GDN_D:2 * GDN_W + (h + 1) * GDN_D]
        qh = qh * lax.rsqrt(jnp.sum(qh * qh, axis=-1, keepdims=True) + EPS) * (GDN_D ** -0.5)
        kh = kh * lax.rsqrt(jnp.sum(kh * kh, axis=-1, keepdims=True) + EPS)
        bh = beta[:, GDN_HEADS + h:GDN_HEADS + h + 1]
        gcol = gc[:, h:h + 1]
        grow = gct[h:h + 1, :]
        decay = jnp.where(incl, jnp.exp(jnp.where(incl, gcol - grow, 0.0)), 0.0)
        kb = kh * bh
        kq = _dot_nt(jnp.concatenate([kb, qh], axis=0), kh)
        lm = jnp.where(strict, kq[:c] * decay, 0.0)
        aqk = kq[c:] * decay
        tinv = _unit_lower_inverse(lm, ii, jj)
        rhs = jnp.concatenate([vh * bh, kb * eg[:, h:h + 1]], axis=1)
        sol = _dot_hi(tinv, rhs)
        u = sol[:, :GDN_D]
        w = sol[:, GDN_D:]
        state = s_ref[h]
        ws = _dot(jnp.concatenate([w, qh * eg[:, h:h + 1]], axis=0), state)
        v_new = u - ws[:c]
        oh = ws[c:] + _dot(aqk, v_new)
        s_ref[h] = state * tail[:, h:h + 1] + _dot_tn(kh * egl[:, h:h + 1], v_new)
        zh = z_ref[:, sl]
        oh = oh * lax.rsqrt(jnp.mean(oh * oh, axis=-1, keepdims=True) + EPS) * gnw
        o_ref[:, sl] = oh * (zh * _sigmoid(zh))


def _gdn(proj, ab, conv_w, alog_pad, dtb_pad, gdn_norm):
    m = proj.shape[0]
    c = CHUNK
    zblk = COL_Z // GDN_W
    return pl.pallas_call(
        _gdn_kernel,
        grid=(m // c,),
        in_specs=[
            pl.BlockSpec((c, 3 * GDN_W), lambda t: (t, 0)),
            pl.BlockSpec((c, GDN_W), lambda t: (t, zblk)),
            pl.BlockSpec((c, LANES), lambda t: (t, 0)),
            pl.BlockSpec((CONV_K, 3 * GDN_W), lambda t: (0, 0)),
            pl.BlockSpec((1, LANES), lambda t: (0, 0)),
            pl.BlockSpec((1, LANES), lambda t: (0, 0)),
            pl.BlockSpec((1, GDN_D), lambda t: (0, 0)),
        ],
        out_specs=pl.BlockSpec((c, GDN_W), lambda t: (t, 0)),
        out_shape=jax.ShapeDtypeStruct((m, GDN_W), F32),
        scratch_shapes=[pltpu.VMEM((HIST + c, 3 * GDN_W), F32),
                        pltpu.VMEM((GDN_HEADS, GDN_D, GDN_D), F32)],
        compiler_params=_params(("arbitrary",)),
        name="gdn",
    )(proj, proj, ab, conv_w, alog_pad, dtb_pad, gdn_norm)


ATT_BQ = 4 * CHUNK
ATT_NKB = 3


def _attn_kernel(q_ref, k0_ref, k1_ref, k2_ref, v0_ref, v1_ref, v2_ref, qw_ref, kw_ref, bias_ref, o_ref):
    m = pl.program_id(0)
    bq = ATT_BQ
    bk = ATT_NKB * bq
    ii = lax.broadcasted_iota(jnp.int32, (bq, bk), 0)
    jj = lax.broadcasted_iota(jnp.int32, (bq, bk), 1)
    qc = ii >> CHUNK_SHIFT
    kc = jj >> CHUNK_SHIFT
    key_pos = (m - (ATT_NKB - 1)) * bq + jj
    visible = (kc >= qc) & (kc <= qc + LEFT_CHUNKS) & (key_pos >= 0)
    qw = qw_ref[...]
    kw = kw_ref[...]
    for h in range(ATT_HEADS):
        sl = slice(h * ATT_DH, (h + 1) * ATT_DH)
        q = _rmsnorm(q_ref[:, sl], qw)
        k = jnp.concatenate([k0_ref[:, sl], k1_ref[:, sl], k2_ref[:, sl]], axis=0)
        k = _rmsnorm(k, kw)
        v = jnp.concatenate([v0_ref[:, sl], v1_ref[:, sl], v2_ref[:, sl]], axis=0)
        s = _dot_nt(q, k) * (ATT_DH ** -0.5)
        s = jnp.where(visible, s + bias_ref[h], NEG_INF)
        mx = jnp.max(s, axis=-1, keepdims=True)
        p = jnp.exp(s - mx)
        denom = jnp.sum(p, axis=-1, keepdims=True)
        o_ref[:, sl] = _dot(p, v) / denom


def _attn(proj, q_norm, k_norm, bias):
    m = proj.shape[0]
    bq = ATT_BQ
    qb = COL_QKV_B // ATT_W
    kb = qb + 1
    vb = qb + 2

    def kv_spec(col, back):
        return pl.BlockSpec((bq, ATT_W), lambda i: (jnp.maximum(i - back, 0), col))

    return pl.pallas_call(
        _attn_kernel,
        grid=(m // bq,),
        in_specs=[
            pl.BlockSpec((bq, ATT_W), lambda i: (i, qb)),
            kv_spec(kb, 2), kv_spec(kb, 1), kv_spec(kb, 0),
            kv_spec(vb, 2), kv_spec(vb, 1), kv_spec(vb, 0),
            pl.BlockSpec((1, ATT_DH), lambda i: (0, 0)),
            pl.BlockSpec((1, ATT_DH), lambda i: (0, 0)),
            pl.BlockSpec((ATT_HEADS, bq, ATT_NKB * bq), lambda i: (0, 0, 0)),
        ],
        out_specs=pl.BlockSpec((bq, ATT_W), lambda i: (i, 0)),
        out_shape=jax.ShapeDtypeStruct((m, ATT_W), F32),
        compiler_params=_params(("parallel",)),
        name="attn",
    )(proj, proj, proj, proj, proj, proj, proj, q_norm, k_norm, bias)


def _attn_bias(rel_bias):
    qpos = np.arange(ATT_BQ)[:, None]
    kpos = np.arange(ATT_NKB * ATT_BQ)[None, :] - (ATT_NKB - 1) * ATT_BQ
    rel_idx = np.clip(qpos - kpos, -(CHUNK - 1), MAX_REL) + (CHUNK - 1)
    return rel_bias[:, rel_idx].astype(F32)


def _mix_kernel(x_ref, ya_ref, yb_ref, ga_ref, gb_ref, wa_ref, wb_ref, wo_ref, o_ref):
    ta = jnp.dot(ya_ref[...].astype(BF16), wa_ref[...], preferred_element_type=F32)
    tb = jnp.dot(yb_ref[...].astype(BF16), wb_ref[...], preferred_element_type=F32)
    mixed = _sigmoid(ga_ref[...]) * ta + _sigmoid(gb_ref[...]) * tb
    o_ref[...] = x_ref[...] + jnp.dot(mixed.astype(BF16), wo_ref[...], preferred_element_type=F32)


def _mix(x, ya, yb, proj, w_a, w_b, w_o, tm):
    m, d = x.shape
    gblk = COL_GATES // d
    const = lambda i: (0, 0)
    return pl.pallas_call(
        _mix_kernel,
        grid=(m // tm,),
        in_specs=[
            pl.BlockSpec((tm, d), lambda i: (i, 0)),
            pl.BlockSpec((tm, GDN_W), lambda i: (i, 0)),
            pl.BlockSpec((tm, ATT_W), lambda i: (i, 0)),
            pl.BlockSpec((tm, d), lambda i: (i, gblk)),
            pl.BlockSpec((tm, d), lambda i: (i, gblk + 1)),
            _resident(w_a.shape),
            _resident(w_b.shape),
            _resident(w_o.shape),
        ],
        out_specs=pl.BlockSpec((tm, d), lambda i: (i, 0)),
        out_shape=jax.ShapeDtypeStruct((m, d), F32),
        compiler_params=_params(("parallel",)),
        name="mix",
    )(x, ya, yb, proj, proj, w_a, w_b, w_o)


def _ple_kernel(x_ref, p_ref, nw_ref, wg_ref, wp_ref, o_ref):
    x = x_ref[...]
    h = _rmsnorm(x, nw_ref[...]).astype(BF16)
    gate = jnp.dot(h, wg_ref[...], preferred_element_type=F32)
    pp = jnp.dot(p_ref[...].astype(BF16), wp_ref[...], preferred_element_type=F32)
    o_ref[...] = x + _sigmoid(gate) * pp


def _ple(x, p, norm_w, w_g, w_p, tm):
    m, d = x.shape
    const = lambda i: (0, 0)
    return pl.pallas_call(
        _ple_kernel,
        grid=(m // tm,),
        in_specs=[
            pl.BlockSpec((tm, d), lambda i: (i, 0)),
            pl.BlockSpec((tm, p.shape[1]), lambda i: (i, 0)),
            pl.BlockSpec((1, d), const),
            _resident(w_g.shape),
            _resident(w_p.shape),
        ],
        out_specs=pl.BlockSpec((tm, d), lambda i: (i, 0)),
        out_shape=jax.ShapeDtypeStruct((m, d), F32),
        compiler_params=_params(("parallel",)),
        name="ple",
    )(x, p, norm_w, w_g, w_p)


def _pad_lanes(v):
    return jnp.pad(v.astype(F32), (0, LANES - v.shape[0])).reshape(1, LANES)


def _layer(x, p, ffn1_norm, ffn1_w_gu, ffn1_w_down, mix_norm, w_in, conv_w, a_log, dt_bias,
           gdn_norm, q_norm, k_norm, rel_bias, w_branch_a, w_branch_b, w_out,
           ffn2_norm, ffn2_w_gu, ffn2_w_down, ple_norm, ple_gate, ple_proj):
    m, d = x.shape
    tm = min(512, m)
    row = lambda v: v.astype(F32).reshape(1, -1)
    n_ab = 2 * GDN_HEADS
    qkv_b_lo = W_IN_AB + n_ab
    gates_lo = qkv_b_lo + 3 * ATT_W

    x = _ffn(x, row(ffn1_norm), ffn1_w_gu.astype(BF16), ffn1_w_down.astype(BF16), tm, 512)

    w_main = jnp.concatenate([w_in[:, :W_IN_AB], w_in[:, gates_lo:], w_in[:, qkv_b_lo:gates_lo]],
                             axis=1).astype(BF16)
    w_ab = jnp.pad(w_in[:, W_IN_AB:qkv_b_lo], ((0, 0), (0, LANES - n_ab))).astype(BF16)
    proj, ab = _proj(x, row(mix_norm), w_main, w_ab, tm, 1024)

    ya = _gdn(proj, ab, conv_w.astype(F32), _pad_lanes(a_log), _pad_lanes(dt_bias), row(gdn_norm))
    yb = _attn(proj, row(q_norm), row(k_norm), _attn_bias(rel_bias))
    x = _mix(x, ya, yb, proj, w_branch_a.astype(BF16), w_branch_b.astype(BF16),
             w_out.astype(BF16), min(256, m))

    x = _ffn(x, row(ffn2_norm), ffn2_w_gu.astype(BF16), ffn2_w_down.astype(BF16), tm, 512)
    return _ple(x, p, row(ple_norm), ple_gate.astype(BF16), ple_proj.astype(BF16), min(512, m))


def kernel(x, p, ffn1_norm, ffn1_w_gu, ffn1_w_down, mix_norm, w_in, conv_w, a_log, dt_bias, gdn_norm, q_norm, k_norm, rel_bias, w_branch_a, w_branch_b, w_out, ffn2_norm, ffn2_w_gu, ffn2_w_down, ple_norm, ple_gate, ple_proj):
    depth, batch = p.shape[0], x.shape[0]
    outs = []
    for b in range(batch):
        xb = x[b]
        for i in range(depth):
            xb = _layer(xb, p[i, b], ffn1_norm[i], ffn1_w_gu[i], ffn1_w_down[i], mix_norm[i], w_in[i],
                        conv_w[i], a_log[i], dt_bias[i], gdn_norm[i], q_norm[i], k_norm[i], rel_bias[i],
                        w_branch_a[i], w_branch_b[i], w_out[i], ffn2_norm[i], ffn2_w_gu[i],
                        ffn2_w_down[i], ple_norm[i], ple_gate[i], ple_proj[i])
        outs.append(xb)
    return jnp.stack(outs, axis=0)
```
